```python
import jax, jax.numpy as jnp
from jax import lax
import numpy as np

D_MODEL = 2048
BATCH = 2
SEQ = 4096
DEPTH = 1

CHUNK = 64
SG_WIDTH = 1024
SG_GROUPS = 4
SG_GROUP_DIM = SG_WIDTH // SG_GROUPS
SG_BLOCK = 128
ATT_HEADS = 8
ATT_HEAD_DIM = 128
ATT_WIDTH = ATT_HEADS * ATT_HEAD_DIM
LEFT_CHUNKS = 8
BAND = LEFT_CHUNKS + 1
REL_CLIP = 128
MEM_LEN = 256
MEM_HEADS = 4
MEM_HEAD_DIM = D_MODEL // MEM_HEADS
PEER_KEYS = 128
PEER_EXPERTS = PEER_KEYS * PEER_KEYS
PEER_HEADS = 8
PEER_QDIM = 256
PEER_HALF = PEER_QDIM // 2
PEER_TOPK = 16
PEER_BLOCK = 128

EPS = 1e-6
NEG_INF = -1e30
IN_SPLITS = [SG_WIDTH, 2 * SG_WIDTH, 2 * SG_WIDTH + ATT_WIDTH, 2 * SG_WIDTH + 2 * ATT_WIDTH,
             2 * SG_WIDTH + 3 * ATT_WIDTH, 2 * SG_WIDTH + 3 * ATT_WIDTH + D_MODEL]
IN_COLS = 2 * SG_WIDTH + 3 * ATT_WIDTH + 2 * D_MODEL

kernel_name = "hybrid_chunk_gmlp_relattn_peer"


def rmsnorm(x, g):
    xf = x.astype(jnp.float32)
    xf = xf * lax.rsqrt(jnp.mean(xf * xf, axis=-1, keepdims=True) + EPS)
    return xf.astype(x.dtype) * g


def layernorm(x, g, b):
    xf = x.astype(jnp.float32)
    mu = jnp.mean(xf, axis=-1, keepdims=True)
    var = jnp.mean(jnp.square(xf - mu), axis=-1, keepdims=True)
    return ((xf - mu) * lax.rsqrt(var + EPS)).astype(x.dtype) * g + b


def spatial_gating_mixer(z_u, z_v, ln_g, ln_b, w_s, b_s):
    B_, S_, _ = z_u.shape
    nb = S_ // SG_BLOCK
    v = layernorm(z_v, ln_g, ln_b).reshape(B_, nb, SG_BLOCK, SG_GROUPS, SG_GROUP_DIM)
    cpos = jnp.arange(SG_BLOCK) // CHUNK
    mask = (cpos[:, None] >= cpos[None, :]).astype(w_s.dtype)
    w = w_s * mask[None]
    s = jnp.einsum('gij,bnjgc->bnigc', w, v) + jnp.transpose(b_s)[:, :, None]
    return z_u * s.reshape(B_, S_, SG_WIDTH)


def chunked_relpos_attention(q, k, v, rel_bias):
    B_, S_, _ = q.shape
    nc = S_ // CHUNK

    def heads(t):
        return t.reshape(B_, nc, CHUNK, ATT_HEADS, ATT_HEAD_DIM)

    q, k, v = heads(q), heads(k), heads(v)
    pad = ((0, 0), (LEFT_CHUNKS, 0), (0, 0), (0, 0), (0, 0))
    kp, vp = jnp.pad(k, pad), jnp.pad(v, pad)
    k_band = jnp.concatenate([kp[:, w:w + nc] for w in range(BAND)], axis=2)
    v_band = jnp.concatenate([vp[:, w:w + nc] for w in range(BAND)], axis=2)
    scale = ATT_HEAD_DIM ** -0.5
    scores = jnp.einsum('bcqhd,bckhd->bhcqk', q, k_band).astype(jnp.float32) * scale
    qi = jnp.arange(CHUNK)
    kj = jnp.arange(BAND * CHUNK)
    dist = (LEFT_CHUNKS * CHUNK + qi[:, None]) - kj[None, :]
    idx = jnp.clip(dist, -REL_CLIP, REL_CLIP) + REL_CLIP
    bias = rel_bias[:, idx].astype(jnp.float32)
    scores = scores + bias[None, :, None]
    chunk_ids = jnp.arange(nc)
    valid = (chunk_ids[:, None] - LEFT_CHUNKS + (kj // CHUNK)[None, :]) >= 0
    scores = jnp.where(valid[None, None, :, None, :], scores, NEG_INF)
    p = jax.nn.softmax(scores, axis=-1).astype(v_band.dtype)
    o = jnp.einsum('bhcqk,bckhd->bcqhd', p, v_band)
    return o.reshape(B_, S_, ATT_WIDTH)


def memory_cross_attention(a, m, w_q, w_kv, w_o):
    B_, S_, _ = a.shape
    q = (a @ w_q).reshape(B_, S_, MEM_HEADS, MEM_HEAD_DIM)
    kv = m @ w_kv
    k = kv[..., :D_MODEL].reshape(B_, -1, MEM_HEADS, MEM_HEAD_DIM)
    v = kv[..., D_MODEL:].reshape(B_, -1, MEM_HEADS, MEM_HEAD_DIM)
    s = jnp.einsum('bshd,bmhd->bhsm', q, k).astype(jnp.float32) * (MEM_HEAD_DIM ** -0.5)
    p = jax.nn.softmax(s, axis=-1).astype(v.dtype)
    o = jnp.einsum('bhsm,bmhd->bshd', p, v).reshape(B_, S_, D_MODEL)
    return o @ w_o


def peer_ffn(a, w_pq, sub_keys, expert_u, expert_v):
    B_, S_, D = a.shape
    t = a.reshape(-1, D)
    T = t.shape[0]
    q = (t @ w_pq).reshape(T, PEER_HEADS, 2, PEER_HALF)
    s = jnp.einsum('thpd,pnd->thpn', q, sub_keys).astype(jnp.float32)
    top_s, top_i = lax.top_k(s, PEER_TOPK)
    cand = (top_s[:, :, 0, :, None] + top_s[:, :, 1, None, :]).reshape(T, PEER_HEADS, PEER_TOPK * PEER_TOPK)
    best_s, best_c = lax.top_k(cand, PEER_TOPK)
    i1 = jnp.take_along_axis(top_i[:, :, 0], best_c // PEER_TOPK, axis=-1)
    i2 = jnp.take_along_axis(top_i[:, :, 1], best_c % PEER_TOPK, axis=-1)
    expert = (i1 * PEER_KEYS + i2).astype(jnp.int32)
    gate = jax.nn.softmax(best_s, axis=-1).astype(t.dtype)
    nt = T // PEER_BLOCK

    def block(args):
        xb, eb, gb = args
        u = expert_u[eb]
        pre = jnp.einsum('bd,bed->be', xb, u)
        act = jax.nn.gelu(pre.astype(jnp.float32), approximate=False).astype(xb.dtype) * gb
        return jnp.einsum('be,bed->bd', act, expert_v[eb])

    out = lax.map(block, (t.reshape(nt, PEER_BLOCK, D),
                          expert.reshape(nt, PEER_BLOCK, PEER_HEADS * PEER_TOPK),
                          gate.reshape(nt, PEER_BLOCK, PEER_HEADS * PEER_TOPK)))
    return out.reshape(B_, S_, D)


def _normal(k, shape, std):
    return jax.random.normal(k, shape, jnp.float32) * std


def setup_inputs(seed: int = 0) -> dict:
    key = jax.random.key(seed)
    ks = jax.random.split(key, 23)
    L, D = DEPTH, D_MODEL
    return {
        "x": _normal(ks[0], (BATCH, SEQ, D), 1.0),
        "mem": _normal(ks[1], (BATCH, MEM_LEN, D), 1.0),
        "g_mix": 1.0 + _normal(ks[2], (L, D), 0.02),
        "w_in": _normal(ks[3], (L, D, IN_COLS), D ** -0.5),
        "sg_ln_g": 1.0 + _normal(ks[4], (L, SG_WIDTH), 0.02),
        "sg_ln_b": _normal(ks[5], (L, SG_WIDTH), 0.02),
        "sg_w_s": _normal(ks[6], (L, SG_GROUPS, SG_BLOCK, SG_BLOCK), 0.5 * SG_BLOCK ** -0.5),
        "sg_b_s": 1.0 + _normal(ks[7], (L, SG_GROUPS, SG_BLOCK), 0.02),
        "att_rel_bias": _normal(ks[8], (L, ATT_HEADS, 2 * REL_CLIP + 1), 0.5),
        "w_up_a": _normal(ks[9], (L, SG_WIDTH, D), SG_WIDTH ** -0.5),
        "w_up_b": _normal(ks[10], (L, ATT_WIDTH, D), ATT_WIDTH ** -0.5),
        "w_out": _normal(ks[11], (L, D, D), D ** -0.5),
        "g_mem_q": 1.0 + _normal(ks[12], (L, D), 0.02),
        "g_mem_kv": 1.0 + _normal(ks[13], (L, D), 0.02),
        "mem_w_q": _normal(ks[14], (L, D, D), D ** -0.5),
        "mem_w_kv": _normal(ks[15], (L, D, 2 * D), D ** -0.5),
        "mem_w_o": _normal(ks[16], (L, D, D), D ** -0.5),
        "g_ffn": 1.0 + _normal(ks[17], (L, D), 0.02),
        "peer_w_q": _normal(ks[18], (L, D, PEER_HEADS * PEER_QDIM), D ** -0.5),
        "peer_sub_keys": _normal(ks[19], (L, 2, PEER_KEYS, PEER_HALF), PEER_HALF ** -0.5),
        "peer_u": _normal(ks[20], (L, PEER_EXPERTS, D), D ** -0.5),
        "peer_v": _normal(ks[21], (L, PEER_EXPERTS, D), PEER_HEADS ** -0.5),
        "g_final": 1.0 + _normal(ks[22], (D,), 0.02),
    }


def reference(x, mem, g_mix, w_in, sg_ln_g, sg_ln_b, sg_w_s, sg_b_s, att_rel_bias, w_up_a, w_up_b, w_out,
              g_mem_q, g_mem_kv, mem_w_q, mem_w_kv, mem_w_o, g_ffn, peer_w_q, peer_sub_keys, peer_u, peer_v,
              g_final):
    h = x
    for l in range(DEPTH):
        a = rmsnorm(h, g_mix[l])
        z = a @ w_in[l]
        z_u, z_v, q, k, v, gate_a, gate_b = jnp.split(z, IN_SPLITS, axis=-1)
        y_a = spatial_gating_mixer(jax.nn.gelu(z_u, approximate=False), jax.nn.gelu(z_v, approximate=False),
                                   sg_ln_g[l], sg_ln_b[l], sg_w_s[l], sg_b_s[l])
        y_b = chunked_relpos_attention(q, k, v, att_rel_bias[l])
        merged = jax.nn.sigmoid(gate_a) * (y_a @ w_up_a[l]) + jax.nn.sigmoid(gate_b) * (y_b @ w_up_b[l])
        h = h + merged @ w_out[l]
        h = h + memory_cross_attention(rmsnorm(h, g_mem_q[l]), rmsnorm(mem, g_mem_kv[l]),
                                       mem_w_q[l], mem_w_kv[l], mem_w_o[l])
        h = h + peer_ffn(rmsnorm(h, g_ffn[l]), peer_w_q[l], peer_sub_keys[l], peer_u[l], peer_v[l])
    return rmsnorm(h, g_final)
```

```python
import functools
import math

import jax
import jax.numpy as jnp
from jax import lax
from jax.experimental import pallas as pl
from jax.experimental.pallas import tpu as pltpu

F32 = jnp.float32
BF16 = jnp.bfloat16

EPS = 1e-6
NEG_INF = -1e30

CHUNK = 64
LEFT_CHUNKS = 8
BAND = LEFT_CHUNKS + 1
SG_BLOCK = 128
ATT_HEAD_DIM = 128
MEM_HEADS = 4
PEER_HEADS = 8
PEER_TOPK = 16

V7X_VMEM_BYTES = 64 * 1024 * 1024
VMEM_LIMIT_BYTES = V7X_VMEM_BYTES - 8 * 1024 * 1024

NT_DIMS = (((1,), (1,)), ((), ()))
TN_DIMS = (((0,), (0,)), ((), ()))
SQRT_HALF = math.sqrt(0.5)


def _params(*semantics):
    return pltpu.CompilerParams(dimension_semantics=semantics, vmem_limit_bytes=VMEM_LIMIT_BYTES)


def _gelu(x):
    return 0.5 * x * (1.0 + lax.erf(x * SQRT_HALF))


def _sigmoid(x):
    return 1.0 / (1.0 + jnp.exp(-x))


def _identity(x):
    return x


def _rmsnorm(x, g):
    ms = jnp.mean(x * x, axis=-1, keepdims=True)
    return x * lax.rsqrt(ms + EPS) * g


def _rms_matmul_body(x_ref, g_ref, w_ref, o_ref, a_scr, *, act):
    @pl.when(pl.program_id(1) == 0)
    def _():
        a_scr[...] = _rmsnorm(x_ref[...], g_ref[...]).astype(BF16)

    acc = jnp.dot(a_scr[...], w_ref[...].astype(BF16), preferred_element_type=F32)
    o_ref[...] = act(acc).astype(o_ref.dtype)


def rms_matmul(x, g, w, col0, ncols, act, out_dtype, tm=1024, tn=512):
    t, k = x.shape
    tm = min(tm, t)
    assert t % tm == 0 and col0 % tn == 0 and ncols % tn == 0
    joff = col0 // tn
    return pl.pallas_call(
        functools.partial(_rms_matmul_body, act=act),
        grid=(t // tm, ncols // tn),
        in_specs=[
            pl.BlockSpec((tm, k), lambda i, j: (i, 0)),
            pl.BlockSpec((1, k), lambda i, j: (0, 0)),
            pl.BlockSpec((k, tn), lambda i, j: (0, j + joff)),
        ],
        out_specs=pl.BlockSpec((tm, tn), lambda i, j: (i, j)),
        out_shape=jax.ShapeDtypeStruct((t, ncols), out_dtype),
        scratch_shapes=[pltpu.VMEM((tm, k), BF16)],
        compiler_params=_params("parallel", "arbitrary"),
        name="rms_matmul",
    )(x, g.reshape(1, k), w)


def _band_attention_body(q_ref, k0_ref, k1_ref, k2_ref, v0_ref, v1_ref, v2_ref, bm_ref, o_ref, *, heads, dh):
    j = pl.program_id(1)
    qb = q_ref.shape[0]
    col = lax.broadcasted_iota(jnp.int32, (1, 3 * qb), 1)
    edge = jnp.where(col < qb, jnp.where(j >= 2, 0.0, NEG_INF),
                     jnp.where(col < 2 * qb, jnp.where(j >= 1, 0.0, NEG_INF), 0.0))
    scale = dh ** -0.5
    for h in range(heads):
        sl = slice(h * dh, (h + 1) * dh)
        k = jnp.concatenate([k0_ref[:, sl], k1_ref[:, sl], k2_ref[:, sl]], axis=0)
        v = jnp.concatenate([v0_ref[:, sl], v1_ref[:, sl], v2_ref[:, sl]], axis=0)
        s = lax.dot_general(q_ref[:, sl], k, NT_DIMS, preferred_element_type=F32) * scale
        s = s + bm_ref[h] + edge
        m = jnp.max(s, axis=-1, keepdims=True)
        p = jnp.exp(s - m)
        l = jnp.sum(p, axis=-1, keepdims=True)
        o = jnp.dot(p.astype(BF16), v, preferred_element_type=F32) / l
        o_ref[:, sl] = o.astype(o_ref.dtype)


def band_attention(qkv, bias_mask, batch, heads):
    t = qkv.shape[0]
    dh = ATT_HEAD_DIM
    width = heads * dh
    qb = bias_mask.shape[1]
    nb = t // batch // qb

    def kv_spec(back, col):
        return pl.BlockSpec((qb, width), lambda b, j: (b * nb + jnp.maximum(j - back, 0), col))

    return pl.pallas_call(
        functools.partial(_band_attention_body, heads=heads, dh=dh),
        grid=(batch, nb),
        in_specs=[
            pl.BlockSpec((qb, width), lambda b, j: (b * nb + j, 0)),
            kv_spec(2, 1), kv_spec(1, 1), kv_spec(0, 1),
            kv_spec(2, 2), kv_spec(1, 2), kv_spec(0, 2),
            pl.BlockSpec(bias_mask.shape, lambda b, j: (0, 0, 0)),
        ],
        out_specs=pl.BlockSpec((qb, width), lambda b, j: (b * nb + j, 0)),
        out_shape=jax.ShapeDtypeStruct((t, width), BF16),
        compiler_params=_params("parallel", "arbitrary"),
        name="band_attention",
    )(qkv, qkv, qkv, qkv, qkv, qkv, qkv, bias_mask)


def _band_bias_mask(rel_bias, qb):
    rel_clip = (rel_bias.shape[1] - 1) // 2
    r = jnp.arange(qb)[:, None]
    c = jnp.arange(3 * qb)[None, :]
    dist = 2 * qb + r - c
    off = c - (r // CHUNK) * CHUNK
    valid = (off >= 0) & (off < BAND * CHUNK)
    idx = jnp.clip(dist, -rel_clip, rel_clip) + rel_clip
    return jnp.where(valid[None], rel_bias[:, idx].astype(F32), NEG_INF)


def _sg_merge_body(u_ref, v_ref, lng_ref, lnb_ref, ws_ref, bst_ref, yb_ref, wa_ref, wb_ref, ga_ref, gb_ref,
                   o_ref, vn_scr, ya_scr, *, groups):
    tm, width = u_ref.shape
    blk = ws_ref.shape[1]
    gd = width // groups

    @pl.when(pl.program_id(1) == 0)
    def _():
        v = v_ref[...]
        mu = jnp.mean(v, axis=-1, keepdims=True)
        d = v - mu
        var = jnp.mean(d * d, axis=-1, keepdims=True)
        vn_scr[...] = (d * lax.rsqrt(var + EPS) * lng_ref[...] + lnb_ref[...]).astype(BF16)
        ri = lax.broadcasted_iota(jnp.int32, (blk, blk), 0) // CHUNK
        ci = lax.broadcasted_iota(jnp.int32, (blk, blk), 1) // CHUNK
        for g in range(groups):
            wm = jnp.where(ri >= ci, ws_ref[g], 0.0).astype(BF16)
            bias = bst_ref[:, g:g + 1]
            cols = slice(g * gd, (g + 1) * gd)
            for n in range(tm // blk):
                rows = slice(n * blk, (n + 1) * blk)
                s = jnp.dot(wm, vn_scr[rows, cols], preferred_element_type=F32) + bias
                ya_scr[rows, cols] = (u_ref[rows, cols] * s).astype(BF16)

    ta = jnp.dot(ya_scr[...], wa_ref[...].astype(BF16), preferred_element_type=F32)
    tb = jnp.dot(yb_ref[...], wb_ref[...].astype(BF16), preferred_element_type=F32)
    o_ref[...] = (ga_ref[...] * ta + gb_ref[...] * tb).astype(o_ref.dtype)


def sg_merge(uv, ln_g, ln_b, w_s, b_s, yb, w_up_a, w_up_b, gates, tm=512, tn=512):
    t = uv.shape[0]
    width = uv.shape[1] // 2
    d = w_up_a.shape[1]
    groups, blk = w_s.shape[0], w_s.shape[1]
    tm = min(tm, t)
    assert t % tm == 0 and tm % blk == 0 and d % tn == 0
    goff = d // tn
    return pl.pallas_call(
        functools.partial(_sg_merge_body, groups=groups),
        grid=(t // tm, d // tn),
        in_specs=[
            pl.BlockSpec((tm, width), lambda i, j: (i, 0)),
            pl.BlockSpec((tm, width), lambda i, j: (i, 1)),
            pl.BlockSpec((1, width), lambda i, j: (0, 0)),
            pl.BlockSpec((1, width), lambda i, j: (0, 0)),
            pl.BlockSpec(w_s.shape, lambda i, j: (0, 0, 0)),
            pl.BlockSpec((blk, groups), lambda i, j: (0, 0)),
            pl.BlockSpec((tm, yb.shape[1]), lambda i, j: (i, 0)),
            pl.BlockSpec((width, tn), lambda i, j: (0, j)),
            pl.BlockSpec((yb.shape[1], tn), lambda i, j: (0, j)),
            pl.BlockSpec((tm, tn), lambda i, j: (i, j)),
            pl.BlockSpec((tm, tn), lambda i, j: (i, j + goff)),
        ],
        out_specs=pl.BlockSpec((tm, tn), lambda i, j: (i, j)),
        out_shape=jax.ShapeDtypeStruct((t, d), BF16),
        scratch_shapes=[pltpu.VMEM((tm, width), BF16), pltpu.VMEM((tm, width), BF16)],
        compiler_params=_params("parallel", "arbitrary"),
        name="sg_merge",
    )(uv, uv, ln_g.reshape(1, width), ln_b.reshape(1, width), w_s, b_s.T, yb, w_up_a, w_up_b, gates, gates)


def _matmul_residual_body(a_ref, w_ref, r_ref, o_ref):
    o_ref[...] = r_ref[...] + jnp.dot(a_ref[...], w_ref[...].astype(BF16), preferred_element_type=F32)


def matmul_residual(a, w, res, tm=1024, tn=512):
    t, k = a.shape
    n = w.shape[1]
    tm = min(tm, t)
    assert t % tm == 0 and n % tn == 0
    return pl.pallas_call(
        _matmul_residual_body,
        grid=(t // tm, n // tn),
        in_specs=[
            pl.BlockSpec((tm, k), lambda i, j: (i, 0)),
            pl.BlockSpec((k, tn), lambda i, j: (0, j)),
            pl.BlockSpec((tm, tn), lambda i, j: (i, j)),
        ],
        out_specs=pl.BlockSpec((tm, tn), lambda i, j: (i, j)),
        out_shape=jax.ShapeDtypeStruct((t, n), F32),
        compiler_params=_params("parallel", "arbitrary"),
        name="matmul_residual",
    )(a, w, res)


def _cross_attention_body(q_ref, k_ref, v_ref, o_ref, *, heads):
    dh = q_ref.shape[1] // heads
    scale = dh ** -0.5
    for h in range(heads):
        sl = slice(h * dh, (h + 1) * dh)
        s = lax.dot_general(q_ref[:, sl], k_ref[:, sl], NT_DIMS, preferred_element_type=F32) * scale
        m = jnp.max(s, axis=-1, keepdims=True)
        p = jnp.exp(s - m)
        l = jnp.sum(p, axis=-1, keepdims=True)
        o = jnp.dot(p.astype(BF16), v_ref[:, sl], preferred_element_type=F32) / l
        o_ref[:, sl] = o.astype(o_ref.dtype)


def cross_attention(q, kv, batch, tq=512):
    t, d = q.shape
    mem_len = kv.shape[0] // batch
    s = t // batch
    tq = min(tq, s)
    nq = s // tq
    return pl.pallas_call(
        functools.partial(_cross_attention_body, heads=MEM_HEADS),
        grid=(batch, nq),
        in_specs=[
            pl.BlockSpec((tq, d), lambda b, i: (b * nq + i, 0)),
            pl.BlockSpec((mem_len, d), lambda b, i: (b, 0)),
            pl.BlockSpec((mem_len, d), lambda b, i: (b, 1)),
        ],
        out_specs=pl.BlockSpec((tq, d), lambda b, i: (b * nq + i, 0)),
        out_shape=jax.ShapeDtypeStruct((t, d), BF16),
        compiler_params=_params("parallel", "arbitrary"),
        name="cross_attention",
    )(q, kv, kv)


def _take_top(work, count, rows, sentinel):
    vals = []
    for it in range(count):
        m = jnp.max(work, axis=0, keepdims=True)
        vals.append(m)
        if it + 1 < count:
            first = jnp.min(jnp.where(work == m, rows, sentinel), axis=0, keepdims=True)
            work = jnp.where(rows == first, -jnp.inf, work)
    return vals


def _peer_select_body(q_ref, keys_ref, s1_ref, s2_ref, e2_ref, aux_ref, *, heads):
    tt = q_ref.shape[0]
    nkeys, half = keys_ref.shape[1], keys_ref.shape[2]
    topk = PEER_TOPK
    key_rows = lax.broadcasted_iota(jnp.int32, (nkeys, tt), 0)
    r16 = lax.broadcasted_iota(jnp.int32, (topk, tt), 0)
    r8 = lax.broadcasted_iota(jnp.int32, (8, tt), 0)
    cand_rows = lax.broadcasted_iota(jnp.int32, (72, tt), 0)
    ninf = -jnp.inf
    keys = [keys_ref[p].astype(BF16) for p in range(2)]
    for h in range(heads):
        score, top = [], []
        for p in range(2):
            c0 = (2 * h + p) * half
            s = lax.dot_general(keys[p], q_ref[:, c0:c0 + half], NT_DIMS, preferred_element_type=F32)
            score.append(s)
            top.append(jnp.concatenate(_take_top(s, topk, key_rows, nkeys), axis=0))
        c1, c2 = top
        c1lo, c2lo = c1[0:8], c2[0:8]
        cand = jnp.concatenate([
            c1[0:1] + c2,
            jnp.where(r16 >= 1, c1 + c2[0:1], ninf),
            jnp.where(r8 >= 1, c1[1:2] + c2lo, ninf),
            jnp.where(r8 >= 2, c1lo + c2[1:2], ninf),
            jnp.where((r8 >= 2) & (r8 <= 4), c1lo + c2[2:3], ninf),
            jnp.where((r8 >= 2) & (r8 <= 3), c1lo + c2[3:4], ninf),
            jnp.where(r8 == 2, c1lo + c2[4:5], ninf),
        ], axis=0)
        best = _take_top(cand, topk, cand_rows, 72)
        z = jnp.ones_like(best[0])
        for kk in range(1, topk):
            z = z + jnp.exp(best[kk] - best[0])
        m1, m2 = c1[0:1], c2[0:1]
        s1_ref[h] = score[0]
        s2_ref[h] = score[1]
        e2_ref[h] = jnp.exp(score[1] - m2)
        aux_ref[h:h + 1, :] = best[topk - 1]
        aux_ref[heads + h:heads + h + 1, :] = m1
        aux_ref[2 * heads + h:2 * heads + h + 1, :] = 1.0 / z


def peer_select(q, sub_keys, tt=256):
    t = q.shape[0]
    heads = PEER_HEADS
    nkeys = sub_keys.shape[1]
    tt = min(tt, t)
    assert t % tt == 0 and PEER_TOPK == 16
    big = jax.ShapeDtypeStruct((heads, nkeys, t), F32)
    big_spec = pl.BlockSpec((heads, nkeys, tt), lambda i: (0, 0, i))
    return pl.pallas_call(
        functools.partial(_peer_select_body, heads=heads),
        grid=(t // tt,),
        in_specs=[
            pl.BlockSpec((tt, q.shape[1]), lambda i: (i, 0)),
            pl.BlockSpec(sub_keys.shape, lambda i: (0, 0, 0)),
        ],
        out_specs=[big_spec, big_spec, big_spec, pl.BlockSpec((3 * heads, tt), lambda i: (0, i))],
        out_shape=[big, big, big, jax.ShapeDtypeStruct((3 * heads, t), F32)],
        compiler_params=_params("parallel"),
        name="peer_select",
    )(q, sub_keys)


def _peer_dense_body(h_ref, gf_ref, gfin_ref, s1_ref, s2_ref, e2_ref, aux_ref, u_ref, v_ref, o_ref,
                     a_scr, pt_scr, at_scr, *, heads, final_norm, row_chunk):
    j = pl.program_id(1)
    te = u_ref.shape[0]
    nkeys = s1_ref.shape[1]
    nib = te // nkeys

    @pl.when(j == 0)
    def _():
        a_scr[...] = _rmsnorm(h_ref[...], gf_ref[...]).astype(BF16)
        o_ref[...] = jnp.zeros_like(o_ref)

    pt_scr[...] = lax.dot_general(u_ref[...].astype(BF16), a_scr[...], NT_DIMS, preferred_element_type=F32)

    for ib in range(nib):
        i1 = j * nib + ib
        s1b, c1b, taub = [], [], []
        for h in range(heads):
            s1 = s1_ref[h, pl.ds(i1, 1), :]
            s1b.append(s1)
            c1b.append(jnp.exp(s1 - aux_ref[heads + h:heads + h + 1, :]) * aux_ref[2 * heads + h:2 * heads + h + 1, :])
            taub.append(aux_ref[h:h + 1, :])
        for r in range(nkeys // row_chunk):
            rows = slice(r * row_chunk, (r + 1) * row_chunk)
            w = None
            for h in range(heads):
                term = jnp.where(s1b[h] + s2_ref[h, rows, :] >= taub[h], c1b[h] * e2_ref[h, rows, :], 0.0)
                w = term if w is None else w + term
            erows = slice(ib * nkeys + r * row_chunk, ib * nkeys + (r + 1) * row_chunk)
            at_scr[erows, :] = (_gelu(pt_scr[erows, :]) * w).astype(BF16)

    o_ref[...] += lax.dot_general(at_scr[...], v_ref[...].astype(BF16), TN_DIMS, preferred_element_type=F32)

    @pl.when(j == pl.num_programs(1) - 1)
    def _():
        y = h_ref[...] + o_ref[...]
        o_ref[...] = _rmsnorm(y, gfin_ref[...]) if final_norm else y


def peer_dense(h, g_ffn, g_final, s1, s2, e2, aux, expert_u, expert_v, final_norm, tt=512, te=256, row_chunk=32):
    t, d = h.shape
    heads, nkeys = s1.shape[0], s1.shape[1]
    ne = expert_u.shape[0]
    tt = min(tt, t)
    assert t % tt == 0 and ne % te == 0 and te % nkeys == 0 and ne == nkeys * nkeys
    tok_spec = pl.BlockSpec((heads, nkeys, tt), lambda i, j: (0, 0, i))
    return pl.pallas_call(
        functools.partial(_peer_dense_body, heads=heads, final_norm=final_norm, row_chunk=row_chunk),
        grid=(t // tt, ne // te),
        in_specs=[
            pl.BlockSpec((tt, d), lambda i, j: (i, 0)),
            pl.BlockSpec((1, d), lambda i, j: (0, 0)),
            pl.BlockSpec((1, d), lambda i, j: (0, 0)),
            tok_spec, tok_spec, tok_spec,
            pl.BlockSpec((3 * heads, tt), lambda i, j: (0, i)),
            pl.BlockSpec((te, d), lambda i, j: (j, 0)),
            pl.BlockSpec((te, d), lambda i, j: (j, 0)),
        ],
        out_specs=pl.BlockSpec((tt, d), lambda i, j: (i, 0)),
        out_shape=jax.ShapeDtypeStruct((t, d), F32),
        scratch_shapes=[pltpu.VMEM((tt, d), BF16), pltpu.VMEM((te, tt), F32), pltpu.VMEM((te, tt), BF16)],
        compiler_params=_params("parallel", "arbitrary"),
        name="peer_dense",
    )(h, g_ffn.reshape(1, d), g_final.reshape(1, d), s1, s2, e2, aux, expert_u, expert_v)


def kernel(x, mem, g_mix, w_in, sg_ln_g, sg_ln_b, sg_w_s, sg_b_s, att_rel_bias, w_up_a, w_up_b, w_out, g_mem_q, g_mem_kv, mem_w_q, mem_w_kv, mem_w_o, g_ffn, peer_w_q, peer_sub_keys, peer_u, peer_v, g_final):
    batch, seq, d = x.shape
    depth = w_in.shape[0]
    sg_width = sg_ln_g.shape[1]
    heads = att_rel_bias.shape[1]
    att_width = heads * ATT_HEAD_DIM
    qb = LEFT_CHUNKS * CHUNK // 2
    assert seq % qb == 0 and qb % CHUNK == 0
    h = x.reshape(batch * seq, d)
    mem2 = mem.reshape(batch * mem.shape[1], d)
    for l in range(depth):
        uv = rms_matmul(h, g_mix[l], w_in[l], 0, 2 * sg_width, _gelu, F32)
        qkv = rms_matmul(h, g_mix[l], w_in[l], 2 * sg_width, 3 * att_width, _identity, BF16)
        gates = rms_matmul(h, g_mix[l], w_in[l], 2 * sg_width + 3 * att_width, 2 * d, _sigmoid, F32)
        yb = band_attention(qkv, _band_bias_mask(att_rel_bias[l], qb), batch, heads)
        merged = sg_merge(uv, sg_ln_g[l], sg_ln_b[l], sg_w_s[l], sg_b_s[l], yb, w_up_a[l], w_up_b[l], gates)
        h = matmul_residual(merged, w_out[l], h)
        qm = rms_matmul(h, g_mem_q[l], mem_w_q[l], 0, d, _identity, BF16)
        kv = rms_matmul(mem2, g_mem_kv[l], mem_w_kv[l], 0, 2 * d, _identity, BF16)
        om = cross_attention(qm, kv, batch)
        h = matmul_residual(om, mem_w_o[l], h)
        pq = rms_matmul(h, g_ffn[l], peer_w_q[l], 0, peer_w_q.shape[2], _identity, BF16)
        s1, s2, e2, aux = peer_select(pq, peer_sub_keys[l])
        h = peer_dense(h, g_ffn[l], g_final, s1, s2, e2, aux, peer_u[l], peer_v[l], final_norm=(l == depth - 1))
    return h.reshape(batch, seq, d)
```

```python
import functools
import math

import jax
import jax.numpy as jnp
from jax import lax
from jax.experimental import pallas as pl
from jax.experimental.pallas import tpu as pltpu

F32 = jnp.float32
BF16 = jnp.bfloat16

EPS = 1e-6
NEG_INF = -1e30

CHUNK = 64
LEFT_CHUNKS = 8
BAND = LEFT_CHUNKS + 1
SG_BLOCK = 128
ATT_HEAD_DIM = 128
MEM_HEADS = 4
PEER_HEADS = 8
PEER_TOPK = 16

V7X_VMEM_BYTES = 64 * 1024 * 1024
VMEM_LIMIT_BYTES = V7X_VMEM_BYTES - 4 * 1024 * 1024

NT_DIMS = (((1,), (1,)), ((), ()))
TN_DIMS = (((0,), (0,)), ((), ()))
SQRT_HALF = math.sqrt(0.5)


def _params(*semantics):
    return pltpu.CompilerParams(dimension_semantics=semantics, vmem_limit_bytes=VMEM_LIMIT_BYTES)


def _gelu(x):
    return 0.5 * x * (1.0 + lax.erf(x * SQRT_HALF))


def _sigmoid(x):
    return 1.0 / (1.0 + jnp.exp(-x))


def _identity(x):
    return x


def _rmsnorm(x, g):
    ms = jnp.mean(x * x, axis=-1, keepdims=True)
    return x * lax.rsqrt(ms + EPS) * g


def _rms_matmul_body(x_ref, g_ref, w_ref, o_ref, a_scr, *, act):
    @pl.when(pl.program_id(1) == 0)
    def _():
        a_scr[...] = _rmsnorm(x_ref[...], g_ref[...]).astype(BF16)

    acc = jnp.dot(a_scr[...], w_ref[...].astype(BF16), preferred_element_type=F32)
    o_ref[...] = act(acc).astype(o_ref.dtype)


def rms_matmul(x, g, w, col0, ncols, act, out_dtype, tm=1024, tn=512):
    t, k = x.shape
    tm = min(tm, t)
    assert t % tm == 0 and col0 % tn == 0 and ncols % tn == 0
    joff = col0 // tn
    return pl.pallas_call(
        functools.partial(_rms_matmul_body, act=act),
        grid=(t // tm, ncols // tn),
        in_specs=[
            pl.BlockSpec((tm, k), lambda i, j: (i, 0)),
            pl.BlockSpec((1, k), lambda i, j: (0, 0)),
            pl.BlockSpec((k, tn), lambda i, j: (0, j + joff)),
        ],
        out_specs=pl.BlockSpec((tm, tn), lambda i, j: (i, j)),
        out_shape=jax.ShapeDtypeStruct((t, ncols), out_dtype),
        scratch_shapes=[pltpu.VMEM((tm, k), BF16)],
        compiler_params=_params("parallel", "arbitrary"),
        name="rms_matmul",
    )(x, g.reshape(1, k), w)


def _band_attention_body(q_ref, k0_ref, k1_ref, k2_ref, v0_ref, v1_ref, v2_ref, rel_ref, o_ref, bm_ref, *, heads, dh):
    j = pl.program_id(1)
    qb = q_ref.shape[0]

    @pl.when((pl.program_id(0) == 0) & (j == 0))
    def _():
        span = rel_ref.shape[1]
        r = lax.broadcasted_iota(jnp.int32, (qb, 3 * qb), 0)
        c = lax.broadcasted_iota(jnp.int32, (qb, 3 * qb), 1)
        off = c - (r // CHUNK) * CHUNK
        in_band = (off >= 0) & (off < BAND * CHUNK)
        for h in range(heads):
            rows = jnp.broadcast_to(rel_ref[h:h + 1, :], (qb, span))
            toeplitz = pltpu.roll(rows, span - (qb - 1), 1, stride=1, stride_axis=0)
            bm_ref[h] = jnp.where(in_band, toeplitz[:, :3 * qb], NEG_INF)

    col = lax.broadcasted_iota(jnp.int32, (1, 3 * qb), 1)
    edge = jnp.where(col < qb, jnp.where(j >= 2, 0.0, NEG_INF),
                     jnp.where(col < 2 * qb, jnp.where(j >= 1, 0.0, NEG_INF), 0.0))
    scale = dh ** -0.5
    for h in range(heads):
        sl = slice(h * dh, (h + 1) * dh)
        k = jnp.concatenate([k0_ref[:, sl], k1_ref[:, sl], k2_ref[:, sl]], axis=0)
        v = jnp.concatenate([v0_ref[:, sl], v1_ref[:, sl], v2_ref[:, sl]], axis=0)
        s = lax.dot_general(q_ref[:, sl], k, NT_DIMS, preferred_element_type=F32) * scale
        s = s + bm_ref[h] + edge
        m = jnp.max(s, axis=-1, keepdims=True)
        p = jnp.exp(s - m)
        l = jnp.sum(p, axis=-1, keepdims=True)
        o = jnp.dot(p.astype(BF16), v, preferred_element_type=F32) / l
        o_ref[:, sl] = o.astype(o_ref.dtype)


def band_attention(qkv, rel, batch, heads, qb):
    t = qkv.shape[0]
    dh = ATT_HEAD_DIM
    width = heads * dh
    nb = t // batch // qb

    def kv_spec(back, col):
        return pl.BlockSpec((qb, width), lambda b, j: (b * nb + jnp.maximum(j - back, 0), col))

    return pl.pallas_call(
        functools.partial(_band_attention_body, heads=heads, dh=dh),
        grid=(batch, nb),
        in_specs=[
            pl.BlockSpec((qb, width), lambda b, j: (b * nb + j, 0)),
            kv_spec(2, 1), kv_spec(1, 1), kv_spec(0, 1),
            kv_spec(2, 2), kv_spec(1, 2), kv_spec(0, 2),
            pl.BlockSpec(rel.shape, lambda b, j: (0, 0)),
        ],
        out_specs=pl.BlockSpec((qb, width), lambda b, j: (b * nb + j, 0)),
        out_shape=jax.ShapeDtypeStruct((t, width), BF16),
        scratch_shapes=[pltpu.VMEM((heads, qb, 3 * qb), F32)],
        compiler_params=_params("arbitrary", "arbitrary"),
        name="band_attention",
    )(qkv, qkv, qkv, qkv, qkv, qkv, qkv, rel)


def _band_rel_table(rel_bias, qb):
    rel_clip = (rel_bias.shape[1] - 1) // 2
    dist = 3 * qb - 1 - jnp.arange(4 * qb)
    return rel_bias[:, jnp.clip(dist, -rel_clip, rel_clip) + rel_clip].astype(F32)


def _sg_merge_body(u_ref, v_ref, lng_ref, lnb_ref, ws_ref, bst_ref, yb_ref, wa_ref, wb_ref, ga_ref, gb_ref,
                   o_ref, vn_scr, ya_scr, *, groups):
    tm, width = u_ref.shape
    blk = ws_ref.shape[1]
    gd = width // groups

    @pl.when(pl.program_id(1) == 0)
    def _():
        v = v_ref[...]
        mu = jnp.mean(v, axis=-1, keepdims=True)
        d = v - mu
        var = jnp.mean(d * d, axis=-1, keepdims=True)
        vn_scr[...] = (d * lax.rsqrt(var + EPS) * lng_ref[...] + lnb_ref[...]).astype(BF16)
        ri = lax.broadcasted_iota(jnp.int32, (blk, blk), 0) // CHUNK
        ci = lax.broadcasted_iota(jnp.int32, (blk, blk), 1) // CHUNK
        for g in range(groups):
            wm = jnp.where(ri >= ci, ws_ref[g], 0.0).astype(BF16)
            bias = bst_ref[:, g:g + 1]
            cols = slice(g * gd, (g + 1) * gd)
            for n in range(tm // blk):
                rows = slice(n * blk, (n + 1) * blk)
                s = jnp.dot(wm, vn_scr[rows, cols], preferred_element_type=F32) + bias
                ya_scr[rows, cols] = (u_ref[rows, cols] * s).astype(BF16)

    ta = jnp.dot(ya_scr[...], wa_ref[...].astype(BF16), preferred_element_type=F32)
    tb = jnp.dot(yb_ref[...], wb_ref[...].astype(BF16), preferred_element_type=F32)
    o_ref[...] = (ga_ref[...] * ta + gb_ref[...] * tb).astype(o_ref.dtype)


def sg_merge(uv, ln_g, ln_b, w_s, b_s, yb, w_up_a, w_up_b, gates, tm=512, tn=512):
    t = uv.shape[0]
    width = uv.shape[1] // 2
    d = w_up_a.shape[1]
    groups, blk = w_s.shape[0], w_s.shape[1]
    tm = min(tm, t)
    assert t % tm == 0 and tm % blk == 0 and d % tn == 0
    goff = d // tn
    return pl.pallas_call(
        functools.partial(_sg_merge_body, groups=groups),
        grid=(t // tm, d // tn),
        in_specs=[
            pl.BlockSpec((tm, width), lambda i, j: (i, 0)),
            pl.BlockSpec((tm, width), lambda i, j: (i, 1)),
            pl.BlockSpec((1, width), lambda i, j: (0, 0)),
            pl.BlockSpec((1, width), lambda i, j: (0, 0)),
            pl.BlockSpec(w_s.shape, lambda i, j: (0, 0, 0)),
            pl.BlockSpec((blk, groups), lambda i, j: (0, 0)),
            pl.BlockSpec((tm, yb.shape[1]), lambda i, j: (i, 0)),
            pl.BlockSpec((width, tn), lambda i, j: (0, j)),
            pl.BlockSpec((yb.shape[1], tn), lambda i, j: (0, j)),
            pl.BlockSpec((tm, tn), lambda i, j: (i, j)),
            pl.BlockSpec((tm, tn), lambda i, j: (i, j + goff)),
        ],
        out_specs=pl.BlockSpec((tm, tn), lambda i, j: (i, j)),
        out_shape=jax.ShapeDtypeStruct((t, d), BF16),
        scratch_shapes=[pltpu.VMEM((tm, width), BF16), pltpu.VMEM((tm, width), BF16)],
        compiler_params=_params("parallel", "arbitrary"),
        name="sg_merge",
    )(uv, uv, ln_g.reshape(1, width), ln_b.reshape(1, width), w_s, b_s.T, yb, w_up_a, w_up_b, gates, gates)


def _matmul_residual_body(a_ref, w_ref, r_ref, o_ref):
    o_ref[...] = r_ref[...] + jnp.dot(a_ref[...], w_ref[...].astype(BF16), preferred_element_type=F32)


def matmul_residual(a, w, res, tm=1024, tn=512):
    t, k = a.shape
    n = w.shape[1]
    tm = min(tm, t)
    assert t % tm == 0 and n % tn == 0
    return pl.pallas_call(
        _matmul_residual_body,
        grid=(t // tm, n // tn),
        in_specs=[
            pl.BlockSpec((tm, k), lambda i, j: (i, 0)),
            pl.BlockSpec((k, tn), lambda i, j: (0, j)),
            pl.BlockSpec((tm, tn), lambda i, j: (i, j)),
        ],
        out_specs=pl.BlockSpec((tm, tn), lambda i, j: (i, j)),
        out_shape=jax.ShapeDtypeStruct((t, n), F32),
        compiler_params=_params("parallel", "arbitrary"),
        name="matmul_residual",
    )(a, w, res)


def _cross_attention_body(q_ref, k_ref, v_ref, o_ref, *, heads):
    dh = q_ref.shape[1] // heads
    scale = dh ** -0.5
    for h in range(heads):
        sl = slice(h * dh, (h + 1) * dh)
        s = lax.dot_general(q_ref[:, sl], k_ref[:, sl], NT_DIMS, preferred_element_type=F32) * scale
        m = jnp.max(s, axis=-1, keepdims=True)
        p = jnp.exp(s - m)
        l = jnp.sum(p, axis=-1, keepdims=True)
        o = jnp.dot(p.astype(BF16), v_ref[:, sl], preferred_element_type=F32) / l
        o_ref[:, sl] = o.astype(o_ref.dtype)


def cross_attention(q, kv, batch, tq=512):
    t, d = q.shape
    mem_len = kv.shape[0] // batch
    s = t // batch
    tq = min(tq, s)
    nq = s // tq
    return pl.pallas_call(
        functools.partial(_cross_attention_body, heads=MEM_HEADS),
        grid=(batch, nq),
        in_specs=[
            pl.BlockSpec((tq, d), lambda b, i: (b * nq + i, 0)),
            pl.BlockSpec((mem_len, d), lambda b, i: (b, 0)),
            pl.BlockSpec((mem_len, d), lambda b, i: (b, 1)),
        ],
        out_specs=pl.BlockSpec((tq, d), lambda b, i: (b * nq + i, 0)),
        out_shape=jax.ShapeDtypeStruct((t, d), BF16),
        compiler_params=_params("parallel", "arbitrary"),
        name="cross_attention",
    )(q, kv, kv)


def _take_top(work, count, order, sentinel, on_pick):
    vals = []
    for it in range(count):
        m = jnp.max(work, axis=0, keepdims=True)
        vals.append(m)
        first = jnp.min(jnp.where(work == m, order, sentinel), axis=0, keepdims=True)
        pick = order == first
        on_pick(it, pick)
        if it + 1 < count:
            work = jnp.where(pick, -jnp.inf, work)
    return vals


def _peer_select_body(q_ref, keys_ref, c1_ref, nb1_ref, rank2_ref, e2_ref, *, heads):
    tt = q_ref.shape[0]
    nkeys, half = keys_ref.shape[1], keys_ref.shape[2]
    topk = PEER_TOPK
    key_rows = lax.broadcasted_iota(jnp.int32, (nkeys, tt), 0)
    r16 = lax.broadcasted_iota(jnp.int32, (topk, tt), 0)
    r8 = lax.broadcasted_iota(jnp.int32, (8, tt), 0)
    pad = 16 * topk
    in_b0, in_a1, in_b1 = r16 >= 1, r8 >= 1, r8 >= 2
    in_b2, in_b3, in_b4 = (r8 >= 2) & (r8 <= 4), (r8 >= 2) & (r8 <= 3), r8 == 2
    flat = jnp.concatenate([
        r16, jnp.where(in_b0, 16 * r16, pad), jnp.where(in_a1, 16 + r8, pad), jnp.where(in_b1, 16 * r8 + 1, pad),
        jnp.where(in_b2, 16 * r8 + 2, pad), jnp.where(in_b3, 16 * r8 + 3, pad), jnp.where(in_b4, 16 * r8 + 4, pad),
    ], axis=0)
    ncand = flat.shape[0]
    ninf = -jnp.inf
    keys = [keys_ref[p].astype(BF16) for p in range(2)]
    for h in range(heads):
        score, top, rank = [], [], []
        for p in range(2):
            c0 = (2 * h + p) * half
            s = lax.dot_general(keys[p], q_ref[:, c0:c0 + half], NT_DIMS, preferred_element_type=F32)
            rk = [jnp.full((nkeys, tt), float(topk), F32)]

            def note_rank(it, pick, rk=rk):
                rk[0] = jnp.where(pick, float(it), rk[0])

            score.append(s)
            top.append(jnp.concatenate(_take_top(s, topk, key_rows, nkeys, note_rank), axis=0))
            rank.append(rk[0])
        c1, c2 = top
        c1lo, c2lo = c1[0:8], c2[0:8]
        cand = jnp.concatenate([
            c1[0:1] + c2,
            jnp.where(in_b0, c1 + c2[0:1], ninf),
            jnp.where(in_a1, c1[1:2] + c2lo, ninf),
            jnp.where(in_b1, c1lo + c2[1:2], ninf),
            jnp.where(in_b2, c1lo + c2[2:3], ninf),
            jnp.where(in_b3, c1lo + c2[3:4], ninf),
            jnp.where(in_b4, c1lo + c2[4:5], ninf),
        ], axis=0)
        sel = [jnp.zeros((ncand, tt), F32)]

        def note_sel(it, pick, sel=sel):
            sel[0] = jnp.where(pick, 1.0, sel[0])

        best = _take_top(cand, topk, flat, pad, note_sel)
        z = jnp.ones_like(best[0])
        for kk in range(1, topk):
            z = z + jnp.exp(best[kk] - best[0])
        sl = sel[0]
        n_a0 = jnp.sum(sl[0:16], axis=0, keepdims=True)
        n_a1 = jnp.sum(sl[32:40], axis=0, keepdims=True)
        n_lo = sl[16:24] + sl[40:48] + sl[48:56] + sl[56:64] + sl[64:72]
        n_lo = n_lo + jnp.where(r8 == 0, n_a0, 0.0) + jnp.where(r8 == 1, n_a1, 0.0)
        n_by_rank = jnp.concatenate([n_lo, sl[24:32]], axis=0)
        nb1 = jnp.zeros((nkeys, tt), F32)
        for a in range(topk):
            nb1 = jnp.where(rank[0] == float(a), n_by_rank[a:a + 1], nb1)
        c1_ref[h] = jnp.exp(score[0] - c1[0:1]) * (1.0 / z)
        nb1_ref[h] = nb1
        rank2_ref[h] = rank[1]
        e2_ref[h] = jnp.exp(score[1] - c2[0:1])


def peer_select(q, sub_keys, tt=256):
    t = q.shape[0]
    heads = PEER_HEADS
    nkeys = sub_keys.shape[1]
    tt = min(tt, t)
    assert t % tt == 0 and PEER_TOPK == 16
    table = jax.ShapeDtypeStruct((heads, nkeys, t), F32)
    table_spec = pl.BlockSpec((heads, nkeys, tt), lambda i: (0, 0, i))
    return pl.pallas_call(
        functools.partial(_peer_select_body, heads=heads),
        grid=(t // tt,),
        in_specs=[
            pl.BlockSpec((tt, q.shape[1]), lambda i: (i, 0)),
            pl.BlockSpec(sub_keys.shape, lambda i: (0, 0, 0)),
        ],
        out_specs=[table_spec] * 4,
        out_shape=[table] * 4,
        compiler_params=_params("parallel"),
        name="peer_select",
    )(q, sub_keys)


def _peer_dense_body(h_ref, gf_ref, gfin_ref, c1_ref, nb1_ref, rank2_ref, e2_ref, u_ref, vprev_ref, vcur_ref, o_ref,
                     a_scr, acc_scr, pt0_scr, pt1_scr, at0_scr, at1_scr, *, heads, final_norm, row_chunk):
    j = pl.program_id(1)
    n_steps = pl.num_programs(1) - 1
    te = vcur_ref.shape[1]
    nkeys = c1_ref.shape[1]
    nib = te // nkeys
    pair = jnp.minimum(j, n_steps - 1)
    live = (j < n_steps).astype(F32)

    @pl.when(j == 0)
    def _():
        a_scr[...] = _rmsnorm(h_ref[...], gf_ref[...]).astype(BF16)
        acc_scr[...] = jnp.zeros_like(acc_scr)
        at1_scr[...] = jnp.zeros_like(at1_scr)

    def accumulate(at_scr, vt_ref):
        acc_scr[...] += jnp.dot(vt_ref[...], at_scr[...], preferred_element_type=F32)

    def activate(half, pt_scr, at_scr):
        pt_scr[...] = lax.dot_general(u_ref[half * te:(half + 1) * te, :], a_scr[...], NT_DIMS,
                                      preferred_element_type=F32)
        for ib in range(nib):
            i1 = (2 * pair + half) * nib + ib
            c1b = [c1_ref[h, pl.ds(i1, 1), :] * live for h in range(heads)]
            nb1b = [nb1_ref[h, pl.ds(i1, 1), :] for h in range(heads)]
            for r in range(nkeys // row_chunk):
                rows = slice(r * row_chunk, (r + 1) * row_chunk)
                w = None
                for h in range(heads):
                    term = jnp.where(rank2_ref[h, rows, :] < nb1b[h], c1b[h] * e2_ref[h, rows, :], 0.0)
                    w = term if w is None else w + term
                erows = slice(ib * nkeys + r * row_chunk, ib * nkeys + (r + 1) * row_chunk)
                at_scr[erows, :] = (_gelu(pt_scr[erows, :]) * w).astype(BF16)

    accumulate(at1_scr, vprev_ref)
    activate(0, pt0_scr, at0_scr)
    accumulate(at0_scr, vcur_ref)
    activate(1, pt1_scr, at1_scr)

    @pl.when(j == n_steps)
    def _():
        y = h_ref[...] + acc_scr[...].T
        o_ref[...] = _rmsnorm(y, gfin_ref[...]) if final_norm else y


def peer_dense(h, g_ffn, g_final, c1, nb1, rank2, e2, expert_u, expert_vt, final_norm, tt=512, te=512, row_chunk=32):
    t, d = h.shape
    heads, nkeys = c1.shape[0], c1.shape[1]
    ne = expert_u.shape[0]
    tt = min(tt, t)
    assert t % tt == 0 and ne % (2 * te) == 0 and te % nkeys == 0 and ne == nkeys * nkeys
    n_steps = ne // (2 * te)
    tok_spec = pl.BlockSpec((heads, nkeys, tt), lambda i, j: (0, 0, i), pipeline_mode=pl.Buffered(1))
    return pl.pallas_call(
        functools.partial(_peer_dense_body, heads=heads, final_norm=final_norm, row_chunk=row_chunk),
        grid=(t // tt, n_steps + 1),
        in_specs=[
            pl.BlockSpec((tt, d), lambda i, j: (i, 0)),
            pl.BlockSpec((1, d), lambda i, j: (0, 0)),
            pl.BlockSpec((1, d), lambda i, j: (0, 0)),
            tok_spec, tok_spec, tok_spec, tok_spec,
            pl.BlockSpec((2 * te, d), lambda i, j: (jnp.minimum(j, n_steps - 1), 0)),
            pl.BlockSpec((d, te), lambda i, j: (0, jnp.maximum(2 * j - 1, 0))),
            pl.BlockSpec((d, te), lambda i, j: (0, jnp.minimum(2 * j, 2 * n_steps - 1))),
        ],
        out_specs=pl.BlockSpec((tt, d), lambda i, j: (i, 0)),
        out_shape=jax.ShapeDtypeStruct((t, d), F32),
        scratch_shapes=[pltpu.VMEM((tt, d), BF16), pltpu.VMEM((d, tt), F32),
                        pltpu.VMEM((te, tt), F32), pltpu.VMEM((te, tt), F32),
                        pltpu.VMEM((te, tt), BF16), pltpu.VMEM((te, tt), BF16)],
        compiler_params=_params("parallel", "arbitrary"),
        name="peer_dense",
    )(h, g_ffn.reshape(1, d), g_final.reshape(1, d), c1, nb1, rank2, e2, expert_u, expert_vt, expert_vt)


def kernel(x, mem, g_mix, w_in, sg_ln_g, sg_ln_b, sg_w_s, sg_b_s, att_rel_bias, w_up_a, w_up_b, w_out, g_mem_q, g_mem_kv, mem_w_q, mem_w_kv, mem_w_o, g_ffn, peer_w_q, peer_sub_keys, peer_u, peer_v, g_final):
    batch, seq, d = x.shape
    depth = w_in.shape[0]
    sg_width = sg_ln_g.shape[1]
    heads = att_rel_bias.shape[1]
    att_width = heads * ATT_HEAD_DIM
    qb = LEFT_CHUNKS * CHUNK // 2
    assert seq % qb == 0 and qb % CHUNK == 0
    h = x.reshape(batch * seq, d)
    mem2 = mem.reshape(batch * mem.shape[1], d)
    for l in range(depth):
        uv = rms_matmul(h, g_mix[l], w_in[l], 0, 2 * sg_width, _gelu, F32)
        qkv = rms_matmul(h, g_mix[l], w_in[l], 2 * sg_width, 3 * att_width, _identity, BF16)
        gates = rms_matmul(h, g_mix[l], w_in[l], 2 * sg_width + 3 * att_width, 2 * d, _sigmoid, F32)
        yb = band_attention(qkv, _band_rel_table(att_rel_bias[l], qb), batch, heads, qb)
        merged = sg_merge(uv, sg_ln_g[l], sg_ln_b[l], sg_w_s[l], sg_b_s[l], yb, w_up_a[l], w_up_b[l], gates)
        h = matmul_residual(merged, w_out[l], h)
        qm = rms_matmul(h, g_mem_q[l], mem_w_q[l], 0, d, _identity, BF16)
        kv = rms_matmul(mem2, g_mem_kv[l], mem_w_kv[l], 0, 2 * d, _identity, BF16)
        om = cross_attention(qm, kv, batch)
        h = matmul_residual(om, mem_w_o[l], h)
        pq = rms_matmul(h, g_ffn[l], peer_w_q[l], 0, peer_w_q.shape[2], _identity, BF16)
        c1, nb1, rank2, e2 = peer_select(pq, peer_sub_keys[l])
        h = peer_dense(h, g_ffn[l], g_final, c1, nb1, rank2, e2, peer_u[l].astype(BF16), peer_v[l].T.astype(BF16),
                       final_norm=(l == depth - 1))
    return h.reshape(batch, seq, d)
```
